```python
import math
import jax
import jax.numpy as jnp
from jax import lax
import numpy as np

D_MODEL = 4096
BATCH = 2
SEQ = 8192
DEPTH = 2

EPS = 1e-5
HEAD_DIM = 128
BLK = 128

SWA_Q_HEADS = 16
SWA_KV_HEADS = 4
SWA_WINDOW = 128
A_Q = SWA_Q_HEADS * HEAD_DIM
A_KV = SWA_KV_HEADS * HEAD_DIM
A_IN = A_Q + 2 * A_KV

DIL_PATTERNS = ((128, 1), (512, 4), (2048, 16))
DIL_HEADS = 12
B_W = DIL_HEADS * HEAD_DIM
B_IN = len(DIL_PATTERNS) * 3 * B_W

SSM_HEADS = 32
SSM_HEAD_DIM = 64
SSM_D_INNER = SSM_HEADS * SSM_HEAD_DIM
SSM_GROUPS = 8
SSM_STATE = 128
SSM_CONV = 4
SSM_CHUNK = 128
SSM_CONV_DIM = SSM_D_INNER + 2 * SSM_GROUPS * SSM_STATE
C_IN = SSM_D_INNER + SSM_CONV_DIM + SSM_HEADS

N_BRANCH = 3
GATE_IN = N_BRANCH * D_MODEL
IN_TOTAL = A_IN + B_IN + C_IN + GATE_IN
IN_SPLITS = [A_IN, A_IN + B_IN, A_IN + B_IN + C_IN]
BR_A, BR_B, BR_C = A_Q, B_W, SSM_D_INNER
MIX_WIDTH = BR_A + BR_B + BR_C

MEM_LEN = 256
XATTN_HEADS = 4
XATTN_DIM = XATTN_HEADS * HEAD_DIM

D_FF = -(-(8 * D_MODEL) // (3 * 256)) * 256

kernel_name = "hybrid_gated_swa_dilated_ssd_block"


def rms_norm(x, g):
    xf = x.astype(jnp.float32)
    y = xf * lax.rsqrt(jnp.mean(xf * xf, axis=-1, keepdims=True) + EPS)
    return (y * g.astype(jnp.float32)).astype(x.dtype)


def banded_attention(q, k, v, max_dist, sink=None):
    n_, seq, kvh, g, d = q.shape
    nb = seq // BLK
    qb = q.reshape(n_, nb, BLK, kvh, g, d)
    kb = k.reshape(n_, nb, BLK, kvh, d)
    vb = v.reshape(n_, nb, BLK, kvh, d)
    shift = lambda t: jnp.concatenate([jnp.zeros_like(t[:, :1]), t[:, :-1]], axis=1)
    kband = jnp.concatenate([shift(kb), kb], axis=2)
    vband = jnp.concatenate([shift(vb), vb], axis=2)
    s = jnp.einsum('nbqhgd,nbkhd->nbhgqk', qb, kband,
                   preferred_element_type=jnp.float32) * (d ** -0.5)
    qi = jnp.arange(BLK)[:, None]
    kj = jnp.arange(2 * BLK)[None, :]
    dist = BLK + qi - kj
    allowed = (dist >= 0) & (dist <= max_dist)
    not_first = jnp.arange(nb)[:, None, None] > 0
    allowed = allowed[None] & (not_first | (kj >= BLK)[None])
    s = jnp.where(allowed[None, :, None, None], s, -jnp.inf)
    if sink is None:
        lse = jax.nn.logsumexp(s, axis=-1)
    else:
        sk = jnp.broadcast_to(sink.astype(jnp.float32)[None, None, :, :, None, None],
                              s.shape[:-1] + (1,))
        lse = jax.nn.logsumexp(jnp.concatenate([s, sk], axis=-1), axis=-1)
    p = jnp.exp(s - lse[..., None])
    out = jnp.einsum('nbhgqk,nbkhd->nbqhgd', p.astype(v.dtype), vband)
    out = out.reshape(n_, seq, kvh, g, d)
    lse = jnp.transpose(lse, (0, 1, 4, 2, 3)).reshape(n_, seq, kvh, g)
    return out, lse


def swa_sink_attention(q, k, v, sink):
    b, seq, _ = q.shape
    g = SWA_Q_HEADS // SWA_KV_HEADS
    out, _ = banded_attention(q.reshape(b, seq, SWA_KV_HEADS, g, HEAD_DIM),
                              k.reshape(b, seq, SWA_KV_HEADS, HEAD_DIM),
                              v.reshape(b, seq, SWA_KV_HEADS, HEAD_DIM),
                              SWA_WINDOW - 1, sink.reshape(SWA_KV_HEADS, g))
    return out.reshape(b, seq, A_Q)


def dilated_attention(q, k, v, window, dilation):
    b, seq, h, d = q.shape
    ls = seq // dilation
    lp = -(-ls // BLK) * BLK

    def to_sub(t):
        t = t.reshape(b, ls, dilation, h, d).transpose(0, 2, 1, 3, 4).reshape(b * dilation, ls, h, d)
        return jnp.pad(t, ((0, 0), (0, lp - ls), (0, 0), (0, 0)))

    out, lse = banded_attention(to_sub(q)[:, :, :, None], to_sub(k), to_sub(v), window // dilation)
    out = out[:, :ls, :, 0].reshape(b, dilation, ls, h, d).transpose(0, 2, 1, 3, 4).reshape(b, seq, h, d)
    lse = lse[:, :ls, :, 0].reshape(b, dilation, ls, h).transpose(0, 2, 1, 3).reshape(b, seq, h)
    return out, lse


def dilated_mixture_attention(qkv):
    b, seq, _ = qkv.shape
    parts = qkv.reshape(b, seq, len(DIL_PATTERNS), 3, DIL_HEADS, HEAD_DIM)
    outs, lses = [], []
    for gi, (window, dilation) in enumerate(DIL_PATTERNS):
        o, l = dilated_attention(parts[:, :, gi, 0], parts[:, :, gi, 1], parts[:, :, gi, 2],
                                 window, dilation)
        outs.append(o)
        lses.append(l)
    alpha = jax.nn.softmax(jnp.stack(lses), axis=0)
    out = jnp.sum(alpha[..., None] * jnp.stack(outs).astype(jnp.float32), axis=0)
    return out.reshape(b, seq, B_W).astype(qkv.dtype)


def ssd_scan(x, dt, a, bm, cm):
    b, seq, g, r, p = x.shape
    n = bm.shape[-1]
    c = SSM_CHUNK
    nc = seq // c
    xc = (x * dt[..., None]).reshape(b, nc, c, g, r, p)
    bc = bm.reshape(b, nc, c, g, n)
    cc = cm.reshape(b, nc, c, g, n)
    a_cum = jnp.cumsum((dt * a).reshape(b, nc, c, g, r), axis=2)
    seg = a_cum[:, :, :, None] - a_cum[:, :, None]
    causal = jnp.tril(jnp.ones((c, c), dtype=bool))
    decay = jnp.exp(jnp.where(causal[None, None, :, :, None, None], seg, -jnp.inf))
    cb = jnp.einsum('bzlgn,bzsgn->bzlsg', cc, bc)
    y_diag = jnp.einsum('bzlsg,bzlsgr,bzsgrp->bzlgrp', cb, decay, xc)
    decay_states = jnp.exp(a_cum[:, :, -1:] - a_cum)
    states = jnp.einsum('bzsgn,bzsgr,bzsgrp->bzgrpn', bc, decay_states, xc)
    chunk_decay = jnp.exp(a_cum[:, :, -1])

    def step(h, inp):
        s_c, dec_c = inp
        return dec_c[..., None, None] * h + s_c, h

    h0 = jnp.zeros((b, g, r, p, n), states.dtype)
    _, prev = lax.scan(step, h0, (jnp.moveaxis(states, 1, 0), jnp.moveaxis(chunk_decay, 1, 0)))
    prev = jnp.moveaxis(prev, 0, 1)
    y_off = jnp.einsum('bzlgn,bzgrpn,bzlgr->bzlgrp', cc, prev, jnp.exp(a_cum))
    return (y_diag + y_off).reshape(b, seq, g, r, p)


def mamba2_mixer(zxbcdt, conv_w, conv_b, dt_bias, a_log, d_skip, norm_w):
    b, seq, _ = zxbcdt.shape
    f32 = jnp.float32
    z, xbc, dt = jnp.split(zxbcdt, [SSM_D_INNER, SSM_D_INNER + SSM_CONV_DIM], axis=-1)
    xbc = lax.conv_general_dilated(
        xbc, conv_w.astype(xbc.dtype)[:, None, :], window_strides=(1,),
        padding=[(SSM_CONV - 1, 0)], dimension_numbers=('NWC', 'WIO', 'NWC'),
        feature_group_count=SSM_CONV_DIM) + conv_b
    xbc = jax.nn.silu(xbc)
    xs, bm, cm = jnp.split(xbc, [SSM_D_INNER, SSM_D_INNER + SSM_GROUPS * SSM_STATE], axis=-1)
    r = SSM_HEADS // SSM_GROUPS
    xs = xs.astype(f32).reshape(b, seq, SSM_GROUPS, r, SSM_HEAD_DIM)
    bm = bm.astype(f32).reshape(b, seq, SSM_GROUPS, SSM_STATE)
    cm = cm.astype(f32).reshape(b, seq, SSM_GROUPS, SSM_STATE)
    dt = jax.nn.softplus(dt.astype(f32) + dt_bias.astype(f32)).reshape(b, seq, SSM_GROUPS, r)
    a = -jnp.exp(a_log.astype(f32)).reshape(SSM_GROUPS, r)
    y = ssd_scan(xs, dt, a, bm, cm) + d_skip.astype(f32).reshape(SSM_GROUPS, r)[..., None] * xs
    y = y.reshape(b, seq, SSM_D_INNER) * jax.nn.silu(z.astype(f32))
    yg = y.reshape(b, seq, SSM_GROUPS, SSM_D_INNER // SSM_GROUPS)
    yg = yg * lax.rsqrt(jnp.mean(yg * yg, axis=-1, keepdims=True) + EPS)
    return (yg.reshape(b, seq, SSM_D_INNER) * norm_w.astype(f32)).astype(zxbcdt.dtype)


def hybrid_mixer(h, w_in, attn_sink, conv_w, conv_b, dt_bias, a_log, d_skip, ssm_norm,
                 w_branch, w_out):
    b, seq, _ = h.shape
    proj = h @ w_in
    a_qkv, b_qkv, c_in, gate_logits = jnp.split(proj, IN_SPLITS, axis=-1)
    aq, ak, av = jnp.split(a_qkv, [A_Q, A_Q + A_KV], axis=-1)
    y_a = swa_sink_attention(aq, ak, av, attn_sink)
    y_b = dilated_mixture_attention(b_qkv)
    y_c = mamba2_mixer(c_in, conv_w, conv_b, dt_bias, a_log, d_skip, ssm_norm)
    wa, wb, wc = jnp.split(w_branch, [BR_A, BR_A + BR_B], axis=0)
    gates = jax.nn.sigmoid(gate_logits.astype(jnp.float32)).reshape(b, seq, N_BRANCH, D_MODEL)
    merged = (gates[:, :, 0] * (y_a @ wa) + gates[:, :, 1] * (y_b @ wb)
              + gates[:, :, 2] * (y_c @ wc))
    return merged.astype(h.dtype) @ w_out


def memory_cross_attention(h, m, wq, wk, wv, wo):
    b, seq, _ = h.shape
    q = (h @ wq).reshape(b, seq, XATTN_HEADS, HEAD_DIM)
    k = (m @ wk).reshape(b, m.shape[1], XATTN_HEADS, HEAD_DIM)
    v = (m @ wv).reshape(b, m.shape[1], XATTN_HEADS, HEAD_DIM)
    s = jnp.einsum('blhd,bmhd->bhlm', q, k, preferred_element_type=jnp.float32) * (HEAD_DIM ** -0.5)
    p = jax.nn.softmax(s, axis=-1)
    o = jnp.einsum('bhlm,bmhd->blhd', p.astype(v.dtype), v).reshape(b, seq, XATTN_DIM)
    return o @ wo


def swiglu(h, w_gate, w_up, w_down):
    return (jax.nn.silu(h @ w_gate) * (h @ w_up)) @ w_down


def setup_inputs(seed: int = 0) -> dict:
    key = jax.random.key(seed)
    ks = iter(jax.random.split(key, 32))
    f32 = jnp.float32

    def dense(shape, fan_in):
        return jax.random.normal(next(ks), shape, f32) * (fan_in ** -0.5)

    def gain(shape):
        return 1.0 + 0.05 * jax.random.normal(next(ks), shape, f32)

    x = jax.random.normal(next(ks), (BATCH, SEQ, D_MODEL), f32)
    mem = jax.random.normal(next(ks), (BATCH, MEM_LEN, D_MODEL), f32)
    norm_mix = gain((DEPTH, D_MODEL))
    w_in = dense((DEPTH, D_MODEL, IN_TOTAL), D_MODEL)
    attn_sink = 0.5 * jax.random.normal(next(ks), (DEPTH, SWA_Q_HEADS), f32)
    conv_w = dense((DEPTH, SSM_CONV, SSM_CONV_DIM), SSM_CONV)
    conv_b = 0.02 * jax.random.normal(next(ks), (DEPTH, SSM_CONV_DIM), f32)
    u = jax.random.uniform(next(ks), (DEPTH, SSM_HEADS), f32)
    dt = jnp.exp(u * (math.log(0.1) - math.log(0.001)) + math.log(0.001))
    dt_bias = dt + jnp.log(-jnp.expm1(-dt))
    a_log = jnp.log(jax.random.uniform(next(ks), (DEPTH, SSM_HEADS), f32, minval=1.0, maxval=16.0))
    d_skip = 1.0 + 0.1 * jax.random.normal(next(ks), (DEPTH, SSM_HEADS), f32)
    ssm_norm = gain((DEPTH, SSM_D_INNER))
    w_branch = dense((DEPTH, MIX_WIDTH, D_MODEL), MIX_WIDTH)
    w_out = dense((DEPTH, D_MODEL, D_MODEL), D_MODEL)
    norm_x = gain((DEPTH, D_MODEL))
    norm_mem = gain((DEPTH, D_MODEL))
    xattn_q = dense((DEPTH, D_MODEL, XATTN_DIM), D_MODEL)
    xattn_k = dense((DEPTH, D_MODEL, XATTN_DIM), D_MODEL)
    xattn_v = dense((DEPTH, D_MODEL, XATTN_DIM), D_MODEL)
    xattn_o = dense((DEPTH, XATTN_DIM, D_MODEL), XATTN_DIM)
    norm_ffn = gain((DEPTH, D_MODEL))
    ffn_gate = dense((DEPTH, D_MODEL, D_FF), D_MODEL)
    ffn_up = dense((DEPTH, D_MODEL, D_FF), D_MODEL)
    ffn_down = dense((DEPTH, D_FF, D_MODEL), D_FF)
    norm_final = gain((D_MODEL,))
    return {"x": x, "mem": mem, "norm_mix": norm_mix, "w_in": w_in, "attn_sink": attn_sink,
            "conv_w": conv_w, "conv_b": conv_b, "dt_bias": dt_bias, "a_log": a_log,
            "d_skip": d_skip, "ssm_norm": ssm_norm, "w_branch": w_branch, "w_out": w_out,
            "norm_x": norm_x, "norm_mem": norm_mem, "xattn_q": xattn_q, "xattn_k": xattn_k,
            "xattn_v": xattn_v, "xattn_o": xattn_o, "norm_ffn": norm_ffn, "ffn_gate": ffn_gate,
            "ffn_up": ffn_up, "ffn_down": ffn_down, "norm_final": norm_final}


def reference(x, mem, norm_mix, w_in, attn_sink, conv_w, conv_b, dt_bias, a_log, d_skip,
              ssm_norm, w_branch, w_out, norm_x, norm_mem, xattn_q, xattn_k, xattn_v, xattn_o,
              norm_ffn, ffn_gate, ffn_up, ffn_down, norm_final):
    for i in range(DEPTH):
        h = rms_norm(x, norm_mix[i])
        x = x + hybrid_mixer(h, w_in[i], attn_sink[i], conv_w[i], conv_b[i], dt_bias[i],
                             a_log[i], d_skip[i], ssm_norm[i], w_branch[i], w_out[i])
        m = rms_norm(mem, norm_mem[i])
        x = x + memory_cross_attention(rms_norm(x, norm_x[i]), m, xattn_q[i], xattn_k[i],
                                       xattn_v[i], xattn_o[i])
        x = x + swiglu(rms_norm(x, norm_ffn[i]), ffn_gate[i], ffn_up[i], ffn_down[i])
    return rms_norm(x, norm_final)
```

```python
import functools

import jax
import jax.numpy as jnp
from jax import lax
from jax.experimental import pallas as pl
from jax.experimental.pallas import tpu as pltpu

F32 = jnp.float32
BF16 = jnp.bfloat16

EPS = 1e-5
HEAD_DIM = 128
BLK = 128
LANES = 128
NEG = -1e30

SWA_Q_HEADS = 16
SWA_KV_HEADS = 4
SWA_WINDOW = 128
DIL_PATTERNS = ((128, 1), (512, 4), (2048, 16))
DIL_HEADS = 12
SSM_HEADS = 32
SSM_HEAD_DIM = 64
SSM_GROUPS = 8
SSM_STATE = 128
SSM_CONV = 4
SSM_CHUNK = 128
XATTN_HEADS = 4

VMEM_LIMIT = 56 * 2**20


def _params(*sem):
    return pltpu.CompilerParams(dimension_semantics=sem, vmem_limit_bytes=VMEM_LIMIT)


def _rmsnorm_body(x_ref, g_ref, o_ref):
    x = x_ref[...]
    ms = jnp.mean(x * x, axis=-1, keepdims=True)
    o_ref[...] = ((x * lax.rsqrt(ms + EPS)) * g_ref[...]).astype(o_ref.dtype)


def rmsnorm(x, g, out_dtype, tm=256):
    m, d = x.shape
    tm = min(tm, m)
    return pl.pallas_call(
        _rmsnorm_body,
        grid=(m // tm,),
        in_specs=[pl.BlockSpec((tm, d), lambda i: (i, 0)),
                  pl.BlockSpec((1, d), lambda i: (0, 0))],
        out_specs=pl.BlockSpec((tm, d), lambda i: (i, 0)),
        out_shape=jax.ShapeDtypeStruct((m, d), out_dtype),
        compiler_params=_params("parallel"),
        name="rmsnorm",
    )(x, g.reshape(1, d).astype(F32))


def _mm_body(*refs, nk, has_res):
    if has_res:
        a_ref, b_ref, r_ref, o_ref = refs[:4]
    else:
        a_ref, b_ref, o_ref = refs[:3]
        r_ref = None
    part = jnp.dot(a_ref[...], b_ref[...], preferred_element_type=F32)

    def finish(acc):
        if has_res:
            acc = r_ref[...] + acc
        o_ref[...] = acc.astype(o_ref.dtype)

    if nk == 1:
        finish(part)
        return
    acc_ref = refs[-1]
    k = pl.program_id(2)

    @pl.when(k == 0)
    def _():
        acc_ref[...] = part

    @pl.when(jnp.logical_and(k > 0, k < nk - 1))
    def _():
        acc_ref[...] += part

    @pl.when(k == nk - 1)
    def _():
        finish(acc_ref[...] + part)


def matmul(a, b, out_dtype, *, tm, tn, tk=None, residual=None):
    m, kdim = a.shape
    n = b.shape[1]
    tm, tn = min(tm, m), min(tn, n)
    tk = kdim if tk is None else tk
    nk = kdim // tk
    assert m % tm == 0 and n % tn == 0 and kdim % tk == 0
    in_specs = [pl.BlockSpec((tm, tk), lambda i, j, k: (i, k)),
                pl.BlockSpec((tk, tn), lambda i, j, k: (k, j))]
    args = [a, b]
    if residual is not None:
        in_specs.append(pl.BlockSpec((tm, tn), lambda i, j, k: (i, j)))
        args.append(residual)
    scratch = [pltpu.VMEM((tm, tn), F32)] if nk > 1 else []
    return pl.pallas_call(
        functools.partial(_mm_body, nk=nk, has_res=residual is not None),
        grid=(m // tm, n // tn, nk),
        in_specs=in_specs,
        out_specs=pl.BlockSpec((tm, tn), lambda i, j, k: (i, j)),
        out_shape=jax.ShapeDtypeStruct((m, n), out_dtype),
        scratch_shapes=scratch,
        compiler_params=_params("parallel", "parallel", "arbitrary"),
        name="matmul",
    )(*args)


def _swiglu_body(a_ref, bg_ref, bu_ref, o_ref):
    a = a_ref[...]
    g = jnp.dot(a, bg_ref[...], preferred_element_type=F32)
    u = jnp.dot(a, bu_ref[...], preferred_element_type=F32)
    o_ref[...] = ((g * jax.nn.sigmoid(g)) * u).astype(o_ref.dtype)


def matmul_swiglu(a, bg, bu, *, tm, tn):
    m, kdim = a.shape
    n = bg.shape[1]
    tm, tn = min(tm, m), min(tn, n)
    assert m % tm == 0 and n % tn == 0
    return pl.pallas_call(
        _swiglu_body,
        grid=(m // tm, n // tn),
        in_specs=[pl.BlockSpec((tm, kdim), lambda i, j: (i, 0)),
                  pl.BlockSpec((kdim, tn), lambda i, j: (0, j)),
                  pl.BlockSpec((kdim, tn), lambda i, j: (0, j))],
        out_specs=pl.BlockSpec((tm, tn), lambda i, j: (i, j)),
        out_shape=jax.ShapeDtypeStruct((m, n), BF16),
        compiler_params=_params("parallel", "parallel"),
        name="matmul_swiglu",
    )(a, bg, bu)


def _merge_body(ya_ref, yb_ref, yc_ref, wa_ref, wb_ref, wc_ref, ga_ref, gb_ref, gc_ref, o_ref):
    def branch(y_ref, w_ref, g_ref):
        gate = jax.nn.sigmoid(g_ref[...].astype(F32))
        return gate * jnp.dot(y_ref[...], w_ref[...], preferred_element_type=F32)

    merged = branch(ya_ref, wa_ref, ga_ref) + branch(yb_ref, wb_ref, gb_ref)
    o_ref[...] = (merged + branch(yc_ref, wc_ref, gc_ref)).astype(o_ref.dtype)


def gated_merge(ya, yb, yc, wa, wb, wc, proj, gate_offs, *, tm, tn):
    m = ya.shape[0]
    n = wa.shape[1]
    tm, tn = min(tm, m), min(tn, n)
    assert m % tm == 0 and n % tn == 0 and all(o % tn == 0 for o in gate_offs)

    def y_spec(y):
        return pl.BlockSpec((tm, y.shape[1]), lambda i, j: (i, 0))

    def w_spec(w):
        return pl.BlockSpec((w.shape[0], tn), lambda i, j: (0, j))

    def g_spec(off):
        return pl.BlockSpec((tm, tn), lambda i, j: (i, off // tn + j))

    return pl.pallas_call(
        _merge_body,
        grid=(m // tm, n // tn),
        in_specs=[y_spec(ya), y_spec(yb), y_spec(yc), w_spec(wa), w_spec(wb), w_spec(wc),
                  g_spec(gate_offs[0]), g_spec(gate_offs[1]), g_spec(gate_offs[2])],
        out_specs=pl.BlockSpec((tm, tn), lambda i, j: (i, j)),
        out_shape=jax.ShapeDtypeStruct((m, n), BF16),
        compiler_params=_params("parallel", "parallel"),
        name="gated_merge",
    )(ya, yb, yc, wa, wb, wc, proj, proj, proj)


def _band_attn_body(*refs, hq, hkv, max_dist, has_sink, want_lse):
    q_ref, kp_ref, kc_ref, vp_ref, vc_ref = refs[:5]
    pos = 5
    sink_ref = None
    if has_sink:
        sink_ref = refs[pos]
        pos += 1
    o_ref = refs[pos]
    lse_ref = refs[pos + 1] if want_lse else None

    g = hq // hkv
    scale = HEAD_DIM ** -0.5
    qi = lax.broadcasted_iota(jnp.int32, (BLK, BLK), 0)
    kj = lax.broadcasted_iota(jnp.int32, (BLK, BLK), 1)
    prev_lo = jnp.where(pl.program_id(1) == 0, 2 * BLK, BLK - max_dist)
    mask_prev = kj >= qi + prev_lo
    mask_cur = kj <= qi
    nt = (((1,), (1,)), ((), ()))
    lse_tile = jnp.zeros((BLK, LANES), F32)
    for kv in range(hkv):
        cs = slice(kv * HEAD_DIM, (kv + 1) * HEAD_DIM)
        kp, kc, vp, vc = kp_ref[:, cs], kc_ref[:, cs], vp_ref[:, cs], vc_ref[:, cs]
        for gi in range(g):
            h = kv * g + gi
            hs = slice(h * HEAD_DIM, (h + 1) * HEAD_DIM)
            q = q_ref[:, hs]
            sp = lax.dot_general(q, kp, nt, preferred_element_type=F32) * scale
            sc = lax.dot_general(q, kc, nt, preferred_element_type=F32) * scale
            sp = jnp.where(mask_prev, sp, NEG)
            sc = jnp.where(mask_cur, sc, NEG)
            m = jnp.maximum(jnp.max(sp, axis=-1, keepdims=True), jnp.max(sc, axis=-1, keepdims=True))
            if has_sink:
                sink = sink_ref[h:h + 1, 0:1]
                m = jnp.maximum(m, sink)
            pp = jnp.exp(sp - m)
            pc = jnp.exp(sc - m)
            l = jnp.sum(pp, axis=-1, keepdims=True) + jnp.sum(pc, axis=-1, keepdims=True)
            if has_sink:
                l = l + jnp.exp(sink - m)
            o = jnp.dot(pp.astype(BF16), vp, preferred_element_type=F32)
            o = o + jnp.dot(pc.astype(BF16), vc, preferred_element_type=F32)
            o_ref[:, hs] = (o / l).astype(o_ref.dtype)
            if want_lse:
                lane = lax.broadcasted_iota(jnp.int32, (BLK, LANES), 1)
                lse_tile = jnp.where(lane == h, m + jnp.log(l), lse_tile)
    if want_lse:
        lse_ref[...] = lse_tile


def band_attention(proj, batch, seq, *, dilation, q_off, k_off, v_off, hq, hkv, max_dist,
                   sink=None, want_lse=False):
    row = proj.shape[1]
    d = dilation
    ls = seq // d
    nb = ls // BLK
    wq, wkv = hq * HEAD_DIM, hkv * HEAD_DIM
    assert ls % BLK == 0 and q_off % wq == 0 and k_off % wkv == 0 and v_off % wkv == 0
    assert d == 1 or (row % wq == 0 and row % wkv == 0)
    assert hq <= LANES
    p3 = proj.reshape(batch, ls, d * row)

    def col(off, w):
        return lambda r: (r * row + off) // w

    qc, kc, vc = col(q_off, wq), col(k_off, wkv), col(v_off, wkv)
    in_specs = [
        pl.BlockSpec((None, BLK, wq), lambda b, j, r: (b, j, qc(r))),
        pl.BlockSpec((None, BLK, wkv), lambda b, j, r: (b, jnp.maximum(j - 1, 0), kc(r))),
        pl.BlockSpec((None, BLK, wkv), lambda b, j, r: (b, j, kc(r))),
        pl.BlockSpec((None, BLK, wkv), lambda b, j, r: (b, jnp.maximum(j - 1, 0), vc(r))),
        pl.BlockSpec((None, BLK, wkv), lambda b, j, r: (b, j, vc(r))),
    ]
    args = [p3, p3, p3, p3, p3]
    if sink is not None:
        in_specs.append(pl.BlockSpec((hq, LANES), lambda b, j, r: (0, 0)))
        args.append(jnp.broadcast_to(sink.astype(F32)[:, None], (hq, LANES)))
    out_specs = [pl.BlockSpec((None, BLK, wq), lambda b, j, r: (b, j, r))]
    out_shape = [jax.ShapeDtypeStruct((batch, ls, d * wq), BF16)]
    if want_lse:
        out_specs.append(pl.BlockSpec((None, BLK, LANES), lambda b, j, r: (b, j, r)))
        out_shape.append(jax.ShapeDtypeStruct((batch, ls, d * LANES), F32))
    res = pl.pallas_call(
        functools.partial(_band_attn_body, hq=hq, hkv=hkv, max_dist=max_dist,
                          has_sink=sink is not None, want_lse=want_lse),
        grid=(batch, nb, d),
        in_specs=in_specs,
        out_specs=out_specs,
        out_shape=out_shape,
        compiler_params=_params("parallel", "parallel", "parallel"),
        name="band_attention",
    )(*args)
    out = res[0].reshape(batch * seq, wq)
    if want_lse:
        return out, res[1].reshape(batch * seq, LANES)
    return out


def _dil_merge_body(o0_ref, o1_ref, o2_ref, l0_ref, l1_ref, l2_ref, y_ref, *, heads):
    l0, l1, l2 = l0_ref[...], l1_ref[...], l2_ref[...]
    mx = jnp.maximum(jnp.maximum(l0, l1), l2)
    e0, e1, e2 = jnp.exp(l0 - mx), jnp.exp(l1 - mx), jnp.exp(l2 - mx)
    den = e0 + e1 + e2
    a0, a1, a2 = e0 / den, e1 / den, e2 / den
    for h in range(heads):
        hs = slice(h * HEAD_DIM, (h + 1) * HEAD_DIM)
        y = a0[:, h:h + 1] * o0_ref[:, hs].astype(F32) + a1[:, h:h + 1] * o1_ref[:, hs].astype(F32)
        y = y + a2[:, h:h + 1] * o2_ref[:, hs].astype(F32)
        y_ref[:, hs] = y.astype(y_ref.dtype)


def dilated_merge(outs, lses, heads, tm=256):
    m, w = outs[0].shape
    tm = min(tm, m)
    o_spec = pl.BlockSpec((tm, w), lambda i: (i, 0))
    l_spec = pl.BlockSpec((tm, LANES), lambda i: (i, 0))
    return pl.pallas_call(
        functools.partial(_dil_merge_body, heads=heads),
        grid=(m // tm,),
        in_specs=[o_spec] * 3 + [l_spec] * 3,
        out_specs=o_spec,
        out_shape=jax.ShapeDtypeStruct((m, w), BF16),
        compiler_params=_params("parallel"),
        name="dilated_merge",
    )(*outs, *lses)


def _expand_heads(cols, width):
    rows = cols[0].shape[0]
    lane = lax.broadcasted_iota(jnp.int32, (rows, len(cols) * width), 1)
    out = jnp.broadcast_to(cols[-1], lane.shape)
    for r in range(len(cols) - 2, -1, -1):
        out = jnp.where(lane < (r + 1) * width, cols[r], out)
    return out


def _ssd_body(z_ref, xbc_ref, dt_ref, cw_ref, cb_ref, dtb_ref, alog_ref, dsk_ref, nw_ref, y_ref,
              xext_ref, xc_ref, state_ref, *, groups, hpg, hdim, nstate):
    c = SSM_CHUNK
    d_inner = groups * hpg * hdim
    gw = hpg * hdim
    conv_dim = d_inner + 2 * groups * nstate
    tail = 8

    @pl.when(pl.program_id(1) == 0)
    def _():
        xext_ref[0:tail, :] = jnp.zeros((tail, conv_dim), F32)
        state_ref[...] = jnp.zeros_like(state_ref)

    xext_ref[tail:tail + c, :] = xbc_ref[...].astype(F32)
    ct = 512
    for t in range(conv_dim // ct):
        cs = slice(t * ct, (t + 1) * ct)
        acc = cb_ref[:, cs] + cw_ref[0:1, cs] * xext_ref[tail - 3:tail - 3 + c, cs]
        for k in range(1, SSM_CONV):
            acc = acc + cw_ref[k:k + 1, cs] * xext_ref[tail - 3 + k:tail - 3 + k + c, cs]
        xc_ref[:, cs] = acc * jax.nn.sigmoid(acc)
    xext_ref[0:tail, :] = xext_ref[c:c + tail, :]

    dtl = dt_ref[...] + dtb_ref[...]
    dt = jnp.maximum(dtl, 0.0) + jnp.log1p(jnp.exp(-jnp.abs(dtl)))
    a = -jnp.exp(alog_ref[...])
    acum = dt * a
    row = lax.broadcasted_iota(jnp.int32, acum.shape, 0)
    for sh in (1, 2, 4, 8, 16, 32, 64):
        acum = acum + jnp.where(row >= sh, pltpu.roll(acum, sh, axis=0), 0.0)
    acum_t = jnp.transpose(acum)
    eacum = jnp.exp(acum)
    alast = acum[c - 1:c, :]
    dec_s = jnp.exp(alast - acum)
    chunk_dec = jnp.exp(alast)

    li = lax.broadcasted_iota(jnp.int32, (c, c), 0)
    si = lax.broadcasted_iota(jnp.int32, (c, c), 1)
    causal = li >= si
    nt = (((1,), (1,)), ((), ()))
    lane_g = lax.broadcasted_iota(jnp.int32, (c, gw), 1)
    for g in range(groups):
        bm = xc_ref[:, d_inner + g * nstate:d_inner + (g + 1) * nstate]
        cm = xc_ref[:, d_inner + (groups + g) * nstate:d_inner + (groups + g + 1) * nstate]
        cm16 = cm.astype(BF16)
        bm16 = bm.astype(BF16)
        cb = lax.dot_general(cm16, bm16, nt, preferred_element_type=F32)
        heads = [g * hpg + r for r in range(hpg)]
        xg = xc_ref[:, g * gw:(g + 1) * gw]
        xdt = xg * _expand_heads([dt[:, h:h + 1] for h in heads], hdim)
        xdt16 = xdt.astype(BF16)
        ms = []
        for h in heads:
            seg = acum[:, h:h + 1] - acum_t[h:h + 1, :]
            ms.append((cb * jnp.exp(jnp.where(causal, seg, NEG))).astype(BF16))
        ybig = jnp.dot(jnp.concatenate(ms, axis=0), xdt16, preferred_element_type=F32)
        y = ybig[(hpg - 1) * c:hpg * c, :]
        for r in range(hpg - 2, -1, -1):
            y = jnp.where(lane_g < (r + 1) * hdim, ybig[r * c:(r + 1) * c, :], y)
        prev = state_ref[:, g * gw:(g + 1) * gw]
        yoff = jnp.dot(cm16, prev.astype(BF16), preferred_element_type=F32)
        y = y + yoff * _expand_heads([eacum[:, h:h + 1] for h in heads], hdim)
        xdec = (xdt * _expand_heads([dec_s[:, h:h + 1] for h in heads], hdim)).astype(BF16)
        st = jnp.dot(jnp.transpose(bm).astype(BF16), xdec, preferred_element_type=F32)
        cdec = _expand_heads([chunk_dec[:, h:h + 1] for h in heads], hdim)
        state_ref[:, g * gw:(g + 1) * gw] = cdec * prev + st
        gs = slice(g * gw, (g + 1) * gw)
        y = y + dsk_ref[:, gs] * xg
        z = z_ref[:, gs].astype(F32)
        y = y * (z * jax.nn.sigmoid(z))
        y = y * lax.rsqrt(jnp.mean(y * y, axis=-1, keepdims=True) + EPS)
        y_ref[:, gs] = (y * nw_ref[:, gs]).astype(y_ref.dtype)


def ssd_mixer(proj, dt_logits, batch, seq, *, z_off, xbc_off, conv_w, conv_b, dt_bias, a_log, d_skip,
              norm_w, groups, hpg, hdim, nstate):
    row = proj.shape[1]
    c = SSM_CHUNK
    nheads = groups * hpg
    d_inner = nheads * hdim
    conv_dim = d_inner + 2 * groups * nstate
    assert seq % c == 0 and z_off % d_inner == 0 and xbc_off % conv_dim == 0 and nheads <= LANES
    assert nstate == LANES and (hpg * hdim) % LANES == 0
    nc = seq // c
    p3 = proj.reshape(batch, seq, row)
    dt3 = dt_logits.reshape(batch, seq, LANES)

    def pad_heads(v):
        return jnp.zeros((1, LANES), F32).at[0, :nheads].set(v.astype(F32))

    def const(shape):
        return pl.BlockSpec(shape, lambda b, j: (0, 0))

    out = pl.pallas_call(
        functools.partial(_ssd_body, groups=groups, hpg=hpg, hdim=hdim, nstate=nstate),
        grid=(batch, nc),
        in_specs=[pl.BlockSpec((None, c, d_inner), lambda b, j: (b, j, z_off // d_inner)),
                  pl.BlockSpec((None, c, conv_dim), lambda b, j: (b, j, xbc_off // conv_dim)),
                  pl.BlockSpec((None, c, LANES), lambda b, j: (b, j, 0)),
                  const((SSM_CONV, conv_dim)), const((1, conv_dim)), const((1, LANES)),
                  const((1, LANES)), const((1, d_inner)), const((1, d_inner))],
        out_specs=pl.BlockSpec((None, c, d_inner), lambda b, j: (b, j, 0)),
        out_shape=jax.ShapeDtypeStruct((batch, seq, d_inner), BF16),
        scratch_shapes=[pltpu.VMEM((c + 8, conv_dim), F32),
                        pltpu.VMEM((c, conv_dim), F32),
                        pltpu.VMEM((nstate, d_inner), F32)],
        compiler_params=_params("parallel", "arbitrary"),
        name="ssd_mixer",
    )(p3, p3, dt3, conv_w.astype(F32), conv_b.reshape(1, conv_dim).astype(F32), pad_heads(dt_bias),
      pad_heads(a_log), jnp.repeat(d_skip.astype(F32), hdim).reshape(1, d_inner),
      norm_w.reshape(1, d_inner).astype(F32))
    return out.reshape(batch * seq, d_inner)


def _xattn_body(q_ref, k_ref, v_ref, o_ref, *, heads):
    scale = HEAD_DIM ** -0.5
    nt = (((1,), (1,)), ((), ()))
    for h in range(heads):
        hs = slice(h * HEAD_DIM, (h + 1) * HEAD_DIM)
        s = lax.dot_general(q_ref[:, hs], k_ref[:, hs], nt, preferred_element_type=F32) * scale
        m = jnp.max(s, axis=-1, keepdims=True)
        p = jnp.exp(s - m)
        l = jnp.sum(p, axis=-1, keepdims=True)
        o = jnp.dot(p.astype(BF16), v_ref[:, hs], preferred_element_type=F32)
        o_ref[:, hs] = (o / l).astype(o_ref.dtype)


def cross_attention(q, k, v, batch, seq, mem_len, heads, tm=512):
    w = heads * HEAD_DIM
    tm = min(tm, seq)
    assert seq % tm == 0
    out = pl.pallas_call(
        functools.partial(_xattn_body, heads=heads),
        grid=(batch, seq // tm),
        in_specs=[pl.BlockSpec((None, tm, w), lambda b, i: (b, i, 0)),
                  pl.BlockSpec((None, mem_len, w), lambda b, i: (b, 0, 0)),
                  pl.BlockSpec((None, mem_len, w), lambda b, i: (b, 0, 0))],
        out_specs=pl.BlockSpec((None, tm, w), lambda b, i: (b, i, 0)),
        out_shape=jax.ShapeDtypeStruct((batch, seq, w), BF16),
        compiler_params=_params("parallel", "parallel"),
        name="cross_attention",
    )(q.reshape(batch, seq, w), k.reshape(batch, mem_len, w), v.reshape(batch, mem_len, w))
    return out.reshape(batch * seq, w)


def kernel(x, mem, norm_mix, w_in, attn_sink, conv_w, conv_b, dt_bias, a_log, d_skip, ssm_norm, w_branch, w_out, norm_x, norm_mem, xattn_q, xattn_k, xattn_v, xattn_o, norm_ffn, ffn_gate, ffn_up, ffn_down, norm_final):
    batch, seq, d_model = x.shape
    mem_len = mem.shape[1]
    depth = w_in.shape[0]
    a_q = SWA_Q_HEADS * HEAD_DIM
    a_kv = SWA_KV_HEADS * HEAD_DIM
    b_w = DIL_HEADS * HEAD_DIM
    n_dil = len(DIL_PATTERNS)
    d_inner = SSM_HEADS * SSM_HEAD_DIM
    conv_dim = d_inner + 2 * SSM_GROUPS * SSM_STATE
    d_ff = ffn_gate.shape[2]

    r_ak = a_q
    r_av = a_q + a_kv
    r_b = a_q + 2 * a_kv
    r_z = r_b + n_dil * 3 * b_w
    r_xbc = r_z + d_inner
    r_dt = r_xbc + conv_dim
    r_gate = r_dt + SSM_HEADS
    o_q = 0
    o_z = a_q
    o_xbc = o_z + d_inner
    o_gate = o_xbc + conv_dim
    o_ak = o_gate + 3 * d_model
    o_av = o_ak + a_kv
    o_b = o_av + a_kv
    gate_offs = tuple(o_gate + i * d_model for i in range(3))

    x2 = x.reshape(batch * seq, d_model).astype(F32)
    mem2 = mem.reshape(batch * mem_len, d_model).astype(F32)

    for i in range(depth):
        wi = w_in[i]
        w_cat = jnp.concatenate(
            [wi[:, 0:a_q], wi[:, r_z:r_dt], wi[:, r_gate:], wi[:, r_ak:r_b], wi[:, r_b:r_z]],
            axis=1).astype(BF16)
        w_dt = jnp.pad(wi[:, r_dt:r_gate], ((0, 0), (0, LANES - SSM_HEADS))).astype(BF16)

        h = rmsnorm(x2, norm_mix[i], BF16)
        proj = matmul(h, w_cat, BF16, tm=1024, tn=512)
        dt_logits = matmul(h, w_dt, F32, tm=1024, tn=LANES)

        y_a = band_attention(proj, batch, seq, dilation=1, q_off=o_q, k_off=o_ak, v_off=o_av,
                             hq=SWA_Q_HEADS, hkv=SWA_KV_HEADS, max_dist=SWA_WINDOW - 1,
                             sink=attn_sink[i])
        outs, lses = [], []
        for gi, (window, dil) in enumerate(DIL_PATTERNS):
            base = o_b + gi * 3 * b_w
            o, l = band_attention(proj, batch, seq, dilation=dil, q_off=base, k_off=base + b_w,
                                  v_off=base + 2 * b_w, hq=DIL_HEADS, hkv=DIL_HEADS,
                                  max_dist=window // dil, want_lse=True)
            outs.append(o)
            lses.append(l)
        y_b = dilated_merge(outs, lses, DIL_HEADS)
        y_c = ssd_mixer(proj, dt_logits, batch, seq, z_off=o_z, xbc_off=o_xbc, conv_w=conv_w[i],
                        conv_b=conv_b[i], dt_bias=dt_bias[i], a_log=a_log[i], d_skip=d_skip[i],
                        norm_w=ssm_norm[i], groups=SSM_GROUPS, hpg=SSM_HEADS // SSM_GROUPS,
                        hdim=SSM_HEAD_DIM, nstate=SSM_STATE)

        wb = w_branch[i].astype(BF16)
        merged = gated_merge(y_a, y_b, y_c, wb[:a_q], wb[a_q:a_q + b_w], wb[a_q + b_w:], proj,
                             gate_offs, tm=512, tn=512)
        x2 = matmul(merged, w_out[i].astype(BF16), F32, tm=1024, tn=512, residual=x2)

        m_n = rmsnorm(mem2, norm_mem[i], BF16)
        hx = rmsnorm(x2, norm_x[i], BF16)
        xw = XATTN_HEADS * HEAD_DIM
        q = matmul(hx, xattn_q[i].astype(BF16), BF16, tm=1024, tn=xw)
        k = matmul(m_n, xattn_k[i].astype(BF16), BF16, tm=512, tn=xw)
        v = matmul(m_n, xattn_v[i].astype(BF16), BF16, tm=512, tn=xw)
        o = cross_attention(q, k, v, batch, seq, mem_len, XATTN_HEADS)
        x2 = matmul(o, xattn_o[i].astype(BF16), F32, tm=1024, tn=512, residual=x2)

        hf = rmsnorm(x2, norm_ffn[i], BF16)
        act = matmul_swiglu(hf, ffn_gate[i].astype(BF16), ffn_up[i].astype(BF16), tm=1024, tn=256)
        x2 = matmul(act, ffn_down[i].astype(BF16), F32, tm=1024, tn=512, tk=d_ff // 2, residual=x2)

    return rmsnorm(x2, norm_final, F32).reshape(batch, seq, d_model)
```

```python
import functools

import jax
import jax.numpy as jnp
from jax import lax
from jax.experimental import pallas as pl
from jax.experimental.pallas import tpu as pltpu

F32 = jnp.float32
BF16 = jnp.bfloat16

EPS = 1e-5
HEAD_DIM = 128
BLK = 128
LANES = 128
NEG = -1e30

SWA_Q_HEADS = 16
SWA_KV_HEADS = 4
SWA_WINDOW = 128
DIL_PATTERNS = ((128, 1), (512, 4), (2048, 16))
DIL_HEADS = 12
SSM_HEADS = 32
SSM_HEAD_DIM = 64
SSM_GROUPS = 8
SSM_STATE = 128
SSM_CONV = 4
SSM_CHUNK = 128
XATTN_HEADS = 4

VMEM_LIMIT = 56 * 2**20


def _params(*sem):
    return pltpu.CompilerParams(dimension_semantics=sem, vmem_limit_bytes=VMEM_LIMIT)


def _rmsnorm_body(x_ref, g_ref, o_ref):
    x = x_ref[...]
    ms = jnp.mean(x * x, axis=-1, keepdims=True)
    o_ref[...] = ((x * lax.rsqrt(ms + EPS)) * g_ref[...]).astype(o_ref.dtype)


def rmsnorm(x, g, out_dtype, tm=256):
    m, d = x.shape
    tm = min(tm, m)
    return pl.pallas_call(
        _rmsnorm_body,
        grid=(m // tm,),
        in_specs=[pl.BlockSpec((tm, d), lambda i: (i, 0)),
                  pl.BlockSpec((1, d), lambda i: (0, 0))],
        out_specs=pl.BlockSpec((tm, d), lambda i: (i, 0)),
        out_shape=jax.ShapeDtypeStruct((m, d), out_dtype),
        compiler_params=_params("parallel"),
        name="rmsnorm",
    )(x, g.reshape(1, d).astype(F32))


def _mm_body(*refs, nk, has_res):
    if has_res:
        a_ref, b_ref, r_ref, o_ref = refs[:4]
    else:
        a_ref, b_ref, o_ref = refs[:3]
        r_ref = None
    part = jnp.dot(a_ref[...], b_ref[...], preferred_element_type=F32)

    def finish(acc):
        if has_res:
            acc = r_ref[...] + acc
        o_ref[...] = acc.astype(o_ref.dtype)

    if nk == 1:
        finish(part)
        return
    acc_ref = refs[-1]
    k = pl.program_id(2)

    @pl.when(k == 0)
    def _():
        acc_ref[...] = part

    @pl.when(jnp.logical_and(k > 0, k < nk - 1))
    def _():
        acc_ref[...] += part

    @pl.when(k == nk - 1)
    def _():
        finish(acc_ref[...] + part)


def matmul(a, b, out_dtype, *, tm, tn, tk=None, residual=None):
    m, kdim = a.shape
    n = b.shape[1]
    tm, tn = min(tm, m), min(tn, n)
    tk = kdim if tk is None else tk
    nk = kdim // tk
    assert m % tm == 0 and n % tn == 0 and kdim % tk == 0
    in_specs = [pl.BlockSpec((tm, tk), lambda i, j, k: (i, k)),
                pl.BlockSpec((tk, tn), lambda i, j, k: (k, j))]
    args = [a, b]
    if residual is not None:
        in_specs.append(pl.BlockSpec((tm, tn), lambda i, j, k: (i, j)))
        args.append(residual)
    scratch = [pltpu.VMEM((tm, tn), F32)] if nk > 1 else []
    return pl.pallas_call(
        functools.partial(_mm_body, nk=nk, has_res=residual is not None),
        grid=(m // tm, n // tn, nk),
        in_specs=in_specs,
        out_specs=pl.BlockSpec((tm, tn), lambda i, j, k: (i, j)),
        out_shape=jax.ShapeDtypeStruct((m, n), out_dtype),
        scratch_shapes=scratch,
        compiler_params=_params("parallel", "parallel", "arbitrary"),
        name="matmul",
    )(*args)


def _swiglu_body(a_ref, bg_ref, bu_ref, o_ref):
    a = a_ref[...]
    g = jnp.dot(a, bg_ref[...], preferred_element_type=F32)
    u = jnp.dot(a, bu_ref[...], preferred_element_type=F32)
    o_ref[...] = ((g * jax.nn.sigmoid(g)) * u).astype(o_ref.dtype)


def matmul_swiglu(a, bg, bu, *, tm, tn):
    m, kdim = a.shape
    n = bg.shape[1]
    tm, tn = min(tm, m), min(tn, n)
    assert m % tm == 0 and n % tn == 0
    return pl.pallas_call(
        _swiglu_body,
        grid=(m // tm, n // tn),
        in_specs=[pl.BlockSpec((tm, kdim), lambda i, j: (i, 0)),
                  pl.BlockSpec((kdim, tn), lambda i, j: (0, j)),
                  pl.BlockSpec((kdim, tn), lambda i, j: (0, j))],
        out_specs=pl.BlockSpec((tm, tn), lambda i, j: (i, j)),
        out_shape=jax.ShapeDtypeStruct((m, n), BF16),
        compiler_params=_params("parallel", "parallel"),
        name="matmul_swiglu",
    )(a, bg, bu)


def _merge_body(ya_ref, yb_ref, yc_ref, wa_ref, wb_ref, wc_ref, ga_ref, gb_ref, gc_ref, o_ref):
    def branch(y_ref, w_ref, g_ref):
        gate = jax.nn.sigmoid(g_ref[...].astype(F32))
        return gate * jnp.dot(y_ref[...], w_ref[...], preferred_element_type=F32)

    merged = branch(ya_ref, wa_ref, ga_ref) + branch(yb_ref, wb_ref, gb_ref)
    o_ref[...] = (merged + branch(yc_ref, wc_ref, gc_ref)).astype(o_ref.dtype)


def gated_merge(ya, yb, yc, wa, wb, wc, proj, gate_offs, *, tm, tn):
    m = ya.shape[0]
    n = wa.shape[1]
    tm, tn = min(tm, m), min(tn, n)
    assert m % tm == 0 and n % tn == 0 and all(o % tn == 0 for o in gate_offs)

    def y_spec(y):
        return pl.BlockSpec((tm, y.shape[1]), lambda i, j: (i, 0))

    def w_spec(w):
        return pl.BlockSpec((w.shape[0], tn), lambda i, j: (0, j))

    def g_spec(off):
        return pl.BlockSpec((tm, tn), lambda i, j: (i, off // tn + j))

    return pl.pallas_call(
        _merge_body,
        grid=(m // tm, n // tn),
        in_specs=[y_spec(ya), y_spec(yb), y_spec(yc), w_spec(wa), w_spec(wb), w_spec(wc),
                  g_spec(gate_offs[0]), g_spec(gate_offs[1]), g_spec(gate_offs[2])],
        out_specs=pl.BlockSpec((tm, tn), lambda i, j: (i, j)),
        out_shape=jax.ShapeDtypeStruct((m, n), BF16),
        compiler_params=_params("parallel", "parallel"),
        name="gated_merge",
    )(ya, yb, yc, wa, wb, wc, proj, proj, proj)


def _band_attn_body(*refs, hq, hkv, max_dist, has_sink, want_lse):
    q_ref, kp_ref, kc_ref, vp_ref, vc_ref = refs[:5]
    pos = 5
    sink_ref = None
    if has_sink:
        sink_ref = refs[pos]
        pos += 1
    o_ref = refs[pos]
    lse_ref = refs[pos + 1] if want_lse else None

    g = hq // hkv
    scale = HEAD_DIM ** -0.5
    qi = lax.broadcasted_iota(jnp.int32, (BLK, BLK), 0)
    kj = lax.broadcasted_iota(jnp.int32, (BLK, BLK), 1)
    prev_lo = jnp.where(pl.program_id(1) == 0, 2 * BLK, BLK - max_dist)
    mask_prev = kj >= qi + prev_lo
    mask_cur = kj <= qi
    nt = (((1,), (1,)), ((), ()))
    lse_tile = jnp.zeros((BLK, LANES), F32)
    for kv in range(hkv):
        cs = slice(kv * HEAD_DIM, (kv + 1) * HEAD_DIM)
        kp, kc, vp, vc = kp_ref[:, cs], kc_ref[:, cs], vp_ref[:, cs], vc_ref[:, cs]
        for gi in range(g):
            h = kv * g + gi
            hs = slice(h * HEAD_DIM, (h + 1) * HEAD_DIM)
            q = q_ref[:, hs]
            sp = lax.dot_general(q, kp, nt, preferred_element_type=F32) * scale
            sc = lax.dot_general(q, kc, nt, preferred_element_type=F32) * scale
            sp = jnp.where(mask_prev, sp, NEG)
            sc = jnp.where(mask_cur, sc, NEG)
            m = jnp.maximum(jnp.max(sp, axis=-1, keepdims=True), jnp.max(sc, axis=-1, keepdims=True))
            if has_sink:
                sink = sink_ref[h:h + 1, 0:1]
                m = jnp.maximum(m, sink)
            pp = jnp.exp(sp - m)
            pc = jnp.exp(sc - m)
            l = jnp.sum(pp, axis=-1, keepdims=True) + jnp.sum(pc, axis=-1, keepdims=True)
            if has_sink:
                l = l + jnp.exp(sink - m)
            o = jnp.dot(pp.astype(BF16), vp, preferred_element_type=F32)
            o = o + jnp.dot(pc.astype(BF16), vc, preferred_element_type=F32)
            o_ref[:, hs] = (o / l).astype(o_ref.dtype)
            if want_lse:
                lane = lax.broadcasted_iota(jnp.int32, (BLK, LANES), 1)
                lse_tile = jnp.where(lane == h, m + jnp.log(l), lse_tile)
    if want_lse:
        lse_ref[...] = lse_tile


def band_attention(proj, batch, seq, *, dilation, q_off, k_off, v_off, hq, hkv, max_dist,
                   sink=None, want_lse=False):
    row = proj.shape[1]
    d = dilation
    ls = seq // d
    nb = ls // BLK
    wq, wkv = hq * HEAD_DIM, hkv * HEAD_DIM
    assert ls % BLK == 0 and q_off % wq == 0 and k_off % wkv == 0 and v_off % wkv == 0
    assert d == 1 or (row % wq == 0 and row % wkv == 0)
    assert hq <= LANES
    p3 = proj.reshape(batch, ls, d * row)

    def col(off, w):
        return lambda r: (r * row + off) // w

    qc, kc, vc = col(q_off, wq), col(k_off, wkv), col(v_off, wkv)
    in_specs = [
        pl.BlockSpec((None, BLK, wq), lambda b, j, r: (b, j, qc(r))),
        pl.BlockSpec((None, BLK, wkv), lambda b, j, r: (b, jnp.maximum(j - 1, 0), kc(r))),
        pl.BlockSpec((None, BLK, wkv), lambda b, j, r: (b, j, kc(r))),
        pl.BlockSpec((None, BLK, wkv), lambda b, j, r: (b, jnp.maximum(j - 1, 0), vc(r))),
        pl.BlockSpec((None, BLK, wkv), lambda b, j, r: (b, j, vc(r))),
    ]
    args = [p3, p3, p3, p3, p3]
    if sink is not None:
        in_specs.append(pl.BlockSpec((hq, LANES), lambda b, j, r: (0, 0)))
        args.append(jnp.broadcast_to(sink.astype(F32)[:, None], (hq, LANES)))
    out_specs = [pl.BlockSpec((None, BLK, wq), lambda b, j, r: (b, j, r))]
    out_shape = [jax.ShapeDtypeStruct((batch, ls, d * wq), BF16)]
    if want_lse:
        out_specs.append(pl.BlockSpec((None, BLK, LANES), lambda b, j, r: (b, j, r)))
        out_shape.append(jax.ShapeDtypeStruct((batch, ls, d * LANES), F32))
    res = pl.pallas_call(
        functools.partial(_band_attn_body, hq=hq, hkv=hkv, max_dist=max_dist,
                          has_sink=sink is not None, want_lse=want_lse),
        grid=(batch, nb, d),
        in_specs=in_specs,
        out_specs=out_specs,
        out_shape=out_shape,
        compiler_params=_params("parallel", "parallel", "parallel"),
        name="band_attention",
    )(*args)
    out = res[0].reshape(batch * seq, wq)
    if want_lse:
        return out, res[1].reshape(batch * seq, LANES)
    return out


def _dil_merge_body(o0_ref, o1_ref, o2_ref, l0_ref, l1_ref, l2_ref, y_ref, *, heads):
    l0, l1, l2 = l0_ref[...], l1_ref[...], l2_ref[...]
    mx = jnp.maximum(jnp.maximum(l0, l1), l2)
    e0, e1, e2 = jnp.exp(l0 - mx), jnp.exp(l1 - mx), jnp.exp(l2 - mx)
    den = e0 + e1 + e2
    a0, a1, a2 = e0 / den, e1 / den, e2 / den
    for h in range(heads):
        hs = slice(h * HEAD_DIM, (h + 1) * HEAD_DIM)
        y = a0[:, h:h + 1] * o0_ref[:, hs].astype(F32) + a1[:, h:h + 1] * o1_ref[:, hs].astype(F32)
        y = y + a2[:, h:h + 1] * o2_ref[:, hs].astype(F32)
        y_ref[:, hs] = y.astype(y_ref.dtype)


def dilated_merge(outs, lses, heads, tm=256):
    m, w = outs[0].shape
    tm = min(tm, m)
    o_spec = pl.BlockSpec((tm, w), lambda i: (i, 0))
    l_spec = pl.BlockSpec((tm, LANES), lambda i: (i, 0))
    return pl.pallas_call(
        functools.partial(_dil_merge_body, heads=heads),
        grid=(m // tm,),
        in_specs=[o_spec] * 3 + [l_spec] * 3,
        out_specs=o_spec,
        out_shape=jax.ShapeDtypeStruct((m, w), BF16),
        compiler_params=_params("parallel"),
        name="dilated_merge",
    )(*outs, *lses)


def _expand_heads(cols, width):
    rows = cols[0].shape[0]
    lane = lax.broadcasted_iota(jnp.int32, (rows, len(cols) * width), 1)
    out = jnp.broadcast_to(cols[-1], lane.shape)
    for r in range(len(cols) - 2, -1, -1):
        out = jnp.where(lane < (r + 1) * width, cols[r], out)
    return out


def _ssd_body(z_ref, xbc_ref, dt_ref, cw_ref, cb_ref, dtb_ref, alog_ref, dsk_ref, nw_ref, y_ref,
              xext_ref, xc_ref, state_ref, *, groups, hpg, hdim, nstate):
    c = SSM_CHUNK
    d_inner = groups * hpg * hdim
    gw = hpg * hdim
    conv_dim = d_inner + 2 * groups * nstate
    tail = 8

    @pl.when(pl.program_id(1) == 0)
    def _():
        xext_ref[0:tail, :] = jnp.zeros((tail, conv_dim), F32)
        state_ref[...] = jnp.zeros_like(state_ref)

    xext_ref[tail:tail + c, :] = xbc_ref[...].astype(F32)
    ct = 512
    for t in range(conv_dim // ct):
        cs = slice(t * ct, (t + 1) * ct)
        acc = cb_ref[:, cs] + cw_ref[0:1, cs] * xext_ref[tail - 3:tail - 3 + c, cs]
        for k in range(1, SSM_CONV):
            acc = acc + cw_ref[k:k + 1, cs] * xext_ref[tail - 3 + k:tail - 3 + k + c, cs]
        xc_ref[:, cs] = acc * jax.nn.sigmoid(acc)
    xext_ref[0:tail, :] = xext_ref[c:c + tail, :]

    dtl = dt_ref[...] + dtb_ref[...]
    dt = jnp.maximum(dtl, 0.0) + jnp.log1p(jnp.exp(-jnp.abs(dtl)))
    a = -jnp.exp(alog_ref[...])
    acum = dt * a
    row = lax.broadcasted_iota(jnp.int32, acum.shape, 0)
    for sh in (1, 2, 4, 8, 16, 32, 64):
        acum = acum + jnp.where(row >= sh, pltpu.roll(acum, sh, axis=0), 0.0)
    acum_t = jnp.transpose(acum)
    eacum = jnp.exp(acum)
    alast = acum[c - 1:c, :]
    dec_s = jnp.exp(alast - acum)
    chunk_dec = jnp.exp(alast)

    li = lax.broadcasted_iota(jnp.int32, (c, c), 0)
    si = lax.broadcasted_iota(jnp.int32, (c, c), 1)
    causal = li >= si
    nt = (((1,), (1,)), ((), ()))
    lane_g = lax.broadcasted_iota(jnp.int32, (c, gw), 1)
    for g in range(groups):
        bm = xc_ref[:, d_inner + g * nstate:d_inner + (g + 1) * nstate]
        cm = xc_ref[:, d_inner + (groups + g) * nstate:d_inner + (groups + g + 1) * nstate]
        cm16 = cm.astype(BF16)
        bm16 = bm.astype(BF16)
        cb = lax.dot_general(cm16, bm16, nt, preferred_element_type=F32)
        heads = [g * hpg + r for r in range(hpg)]
        xg = xc_ref[:, g * gw:(g + 1) * gw]
        xdt = xg * _expand_heads([dt[:, h:h + 1] for h in heads], hdim)
        xdt16 = xdt.astype(BF16)
        ms = []
        for h in heads:
            seg = acum[:, h:h + 1] - acum_t[h:h + 1, :]
            ms.append((cb * jnp.exp(jnp.where(causal, seg, NEG))).astype(BF16))
        ybig = jnp.dot(jnp.concatenate(ms, axis=0), xdt16, preferred_element_type=F32)
        y = ybig[(hpg - 1) * c:hpg * c, :]
        for r in range(hpg - 2, -1, -1):
            y = jnp.where(lane_g < (r + 1) * hdim, ybig[r * c:(r + 1) * c, :], y)
        prev = state_ref[:, g * gw:(g + 1) * gw]
        yoff = jnp.dot(cm16, prev.astype(BF16), preferred_element_type=F32)
        y = y + yoff * _expand_heads([eacum[:, h:h + 1] for h in heads], hdim)
        xdec = (xdt * _expand_heads([dec_s[:, h:h + 1] for h in heads], hdim)).astype(BF16)
        st = jnp.dot(jnp.transpose(bm).astype(BF16), xdec, preferred_element_type=F32)
        cdec = _expand_heads([chunk_dec[:, h:h + 1] for h in heads], hdim)
        state_ref[:, g * gw:(g + 1) * gw] = cdec * prev + st
        gs = slice(g * gw, (g + 1) * gw)
        y = y + dsk_ref[:, gs] * xg
        z = z_ref[:, gs].astype(F32)
        y = y * (z * jax.nn.sigmoid(z))
        y = y * lax.rsqrt(jnp.mean(y * y, axis=-1, keepdims=True) + EPS)
        y_ref[:, gs] = (y * nw_ref[:, gs]).astype(y_ref.dtype)


def ssd_mixer(proj, dt_logits, batch, seq, *, z_off, xbc_off, conv_w, conv_b, dt_bias, a_log, d_skip,
              norm_w, groups, hpg, hdim, nstate):
    row = proj.shape[1]
    c = SSM_CHUNK
    nheads = groups * hpg
    d_inner = nheads * hdim
    conv_dim = d_inner + 2 * groups * nstate
    assert seq % c == 0 and z_off % d_inner == 0 and xbc_off % conv_dim == 0 and nheads <= LANES
    assert nstate == LANES and (hpg * hdim) % LANES == 0
    nc = seq // c
    p3 = proj.reshape(batch, seq, row)
    dt3 = dt_logits.reshape(batch, seq, LANES)

    def pad_heads(v):
        return jnp.zeros((1, LANES), F32).at[0, :nheads].set(v.astype(F32))

    def const(shape):
        return pl.BlockSpec(shape, lambda b, j: (0, 0))

    out = pl.pallas_call(
        functools.partial(_ssd_body, groups=groups, hpg=hpg, hdim=hdim, nstate=nstate),
        grid=(batch, nc),
        in_specs=[pl.BlockSpec((None, c, d_inner), lambda b, j: (b, j, z_off // d_inner)),
                  pl.BlockSpec((None, c, conv_dim), lambda b, j: (b, j, xbc_off // conv_dim)),
                  pl.BlockSpec((None, c, LANES), lambda b, j: (b, j, 0)),
                  const((SSM_CONV, conv_dim)), const((1, conv_dim)), const((1, LANES)),
                  const((1, LANES)), const((1, d_inner)), const((1, d_inner))],
        out_specs=pl.BlockSpec((None, c, d_inner), lambda b, j: (b, j, 0)),
        out_shape=jax.ShapeDtypeStruct((batch, seq, d_inner), BF16),
        scratch_shapes=[pltpu.VMEM((c + 8, conv_dim), F32),
                        pltpu.VMEM((c, conv_dim), F32),
                        pltpu.VMEM((nstate, d_inner), F32)],
        compiler_params=_params("parallel", "arbitrary"),
        name="ssd_mixer",
    )(p3, p3, dt3, conv_w.astype(F32), conv_b.reshape(1, conv_dim).astype(F32), pad_heads(dt_bias),
      pad_heads(a_log), jnp.repeat(d_skip.astype(F32), hdim).reshape(1, d_inner),
      norm_w.reshape(1, d_inner).astype(F32))
    return out.reshape(batch * seq, d_inner)


def _xattn_body(q_ref, k_ref, v_ref, o_ref, *, heads):
    scale = HEAD_DIM ** -0.5
    nt = (((1,), (1,)), ((), ()))
    for h in range(heads):
        hs = slice(h * HEAD_DIM, (h + 1) * HEAD_DIM)
        s = lax.dot_general(q_ref[:, hs], k_ref[:, hs], nt, preferred_element_type=F32) * scale
        m = jnp.max(s, axis=-1, keepdims=True)
        p = jnp.exp(s - m)
        l = jnp.sum(p, axis=-1, keepdims=True)
        o = jnp.dot(p.astype(BF16), v_ref[:, hs], preferred_element_type=F32)
        o_ref[:, hs] = (o / l).astype(o_ref.dtype)


def cross_attention(q, k, v, batch, seq, mem_len, heads, tm=512):
    w = heads * HEAD_DIM
    tm = min(tm, seq)
    assert seq % tm == 0
    out = pl.pallas_call(
        functools.partial(_xattn_body, heads=heads),
        grid=(batch, seq // tm),
        in_specs=[pl.BlockSpec((None, tm, w), lambda b, i: (b, i, 0)),
                  pl.BlockSpec((None, mem_len, w), lambda b, i: (b, 0, 0)),
                  pl.BlockSpec((None, mem_len, w), lambda b, i: (b, 0, 0))],
        out_specs=pl.BlockSpec((None, tm, w), lambda b, i: (b, i, 0)),
        out_shape=jax.ShapeDtypeStruct((batch, seq, w), BF16),
        compiler_params=_params("parallel", "parallel"),
        name="cross_attention",
    )(q.reshape(batch, seq, w), k.reshape(batch, mem_len, w), v.reshape(batch, mem_len, w))
    return out.reshape(batch * seq, w)


def kernel(x, mem, norm_mix, w_in, attn_sink, conv_w, conv_b, dt_bias, a_log, d_skip, ssm_norm, w_branch, w_out, norm_x, norm_mem, xattn_q, xattn_k, xattn_v, xattn_o, norm_ffn, ffn_gate, ffn_up, ffn_down, norm_final):
    batch, seq, d_model = x.shape
    mem_len = mem.shape[1]
    depth = w_in.shape[0]
    a_q = SWA_Q_HEADS * HEAD_DIM
    a_kv = SWA_KV_HEADS * HEAD_DIM
    b_w = DIL_HEADS * HEAD_DIM
    n_dil = len(DIL_PATTERNS)
    d_inner = SSM_HEADS * SSM_HEAD_DIM
    conv_dim = d_inner + 2 * SSM_GROUPS * SSM_STATE
    d_ff = ffn_gate.shape[2]

    r_ak = a_q
    r_av = a_q + a_kv
    r_b = a_q + 2 * a_kv
    r_z = r_b + n_dil * 3 * b_w
    r_xbc = r_z + d_inner
    r_dt = r_xbc + conv_dim
    r_gate = r_dt + SSM_HEADS
    o_q = 0
    o_z = a_q
    o_xbc = o_z + d_inner
    o_gate = o_xbc + conv_dim
    o_ak = o_gate + 3 * d_model
    o_av = o_ak + a_kv
    o_b = o_av + a_kv
    gate_offs = tuple(o_gate + i * d_model for i in range(3))
    assert DIL_PATTERNS[0][1] == 1 and all(d > 1 for _, d in DIL_PATTERNS[1:])

    x2 = x.reshape(batch * seq, d_model).astype(F32)
    mem2 = mem.reshape(batch * mem_len, d_model).astype(F32)

    for i in range(depth):
        wi = w_in[i]
        g_w = 3 * b_w
        w_cat = jnp.concatenate(
            [wi[:, 0:a_q], wi[:, r_z:r_dt], wi[:, r_gate:], wi[:, r_ak:r_b], wi[:, r_b:r_b + g_w]],
            axis=1).astype(BF16)
        w_dt = jnp.pad(wi[:, r_dt:r_gate], ((0, 0), (0, LANES - SSM_HEADS))).astype(BF16)

        h = rmsnorm(x2, norm_mix[i], BF16)
        proj = matmul(h, w_cat, BF16, tm=1024, tn=512)
        dt_logits = matmul(h, w_dt, F32, tm=1024, tn=LANES)

        y_a = band_attention(proj, batch, seq, dilation=1, q_off=o_q, k_off=o_ak, v_off=o_av,
                             hq=SWA_Q_HEADS, hkv=SWA_KV_HEADS, max_dist=SWA_WINDOW - 1,
                             sink=attn_sink[i])
        outs, lses = [], []
        for gi, (window, dil) in enumerate(DIL_PATTERNS):
            if dil == 1:
                src, base = proj, o_b
            else:
                w_g = wi[:, r_b + gi * g_w:r_b + (gi + 1) * g_w].astype(BF16)
                src, base = matmul(h, w_g, BF16, tm=1024, tn=512), 0
            o, l = band_attention(src, batch, seq, dilation=dil, q_off=base, k_off=base + b_w,
                                  v_off=base + 2 * b_w, hq=DIL_HEADS, hkv=DIL_HEADS,
                                  max_dist=window // dil, want_lse=True)
            outs.append(o)
            lses.append(l)
        y_b = dilated_merge(outs, lses, DIL_HEADS)
        y_c = ssd_mixer(proj, dt_logits, batch, seq, z_off=o_z, xbc_off=o_xbc, conv_w=conv_w[i],
                        conv_b=conv_b[i], dt_bias=dt_bias[i], a_log=a_log[i], d_skip=d_skip[i],
                        norm_w=ssm_norm[i], groups=SSM_GROUPS, hpg=SSM_HEADS // SSM_GROUPS,
                        hdim=SSM_HEAD_DIM, nstate=SSM_STATE)

        wb = w_branch[i].astype(BF16)
        merged = gated_merge(y_a, y_b, y_c, wb[:a_q], wb[a_q:a_q + b_w], wb[a_q + b_w:], proj,
                             gate_offs, tm=512, tn=512)
        x2 = matmul(merged, w_out[i].astype(BF16), F32, tm=1024, tn=512, residual=x2)

        m_n = rmsnorm(mem2, norm_mem[i], BF16)
        hx = rmsnorm(x2, norm_x[i], BF16)
        xw = XATTN_HEADS * HEAD_DIM
        q = matmul(hx, xattn_q[i].astype(BF16), BF16, tm=1024, tn=xw)
        k = matmul(m_n, xattn_k[i].astype(BF16), BF16, tm=512, tn=xw)
        v = matmul(m_n, xattn_v[i].astype(BF16), BF16, tm=512, tn=xw)
        o = cross_attention(q, k, v, batch, seq, mem_len, XATTN_HEADS)
        x2 = matmul(o, xattn_o[i].astype(BF16), F32, tm=1024, tn=512, residual=x2)

        hf = rmsnorm(x2, norm_ffn[i], BF16)
        act = matmul_swiglu(hf, ffn_gate[i].astype(BF16), ffn_up[i].astype(BF16), tm=1024, tn=256)
        x2 = matmul(act, ffn_down[i].astype(BF16), F32, tm=1024, tn=512, tk=d_ff // 2, residual=x2)

    return rmsnorm(x2, norm_final, F32).reshape(batch, seq, d_model)
```
